```python
import jax, jax.numpy as jnp
from jax import lax
import numpy as np

D_MODEL = 2048
BATCH = 4
SEQ = 4096
DEPTH = 2
DEC_BATCH = 128
DEC_SEQ = 1
PAST_LEN = 16384
PAGE_SIZE = 128

H_A = D_MODEL // 128
DH_A = 64
ROT_A = DH_A // 4
THETA_A = 500000.0
CMP_BLOCK = 32
CMP_STRIDE = 16
CMP_HID = 2 * DH_A
SEL_BLOCK = 64
N_SEL = 16
WINDOW = 512
H_B = D_MODEL // 256
NOPE_B = 128
ROPE_B = 64
DV_B = 128
Q_LORA = D_MODEL // 4
KV_RANK = D_MODEL // 16
THETA_B = 10000.0
MLA_SCALE = (NOPE_B + ROPE_B) ** -0.5
N_GROUPS = 4
EXPERTS_PER_GROUP = 8
N_EXPERTS = N_GROUPS * EXPERTS_PER_GROUP
TOP_K = 2
D_FF = D_MODEL // 4
MOE_BLOCK = 128
Q_BLOCK = 128
ALPHA = (2.0 * DEPTH) ** 0.25
BETA = (8.0 * DEPTH) ** -0.25
NEG_INF = -1e30
FORCE_BONUS = 1e6
IN_SIZES = (H_A * DH_A, 6 * DH_A, 3 * H_A, Q_LORA, KV_RANK, ROPE_B, 2 * D_MODEL)
IN_COLS = sum(IN_SIZES)
IN_SPLITS = [int(v) for v in np.cumsum(IN_SIZES)[:-1]]

kernel_name = 'nsa_mla_hmoe_deepnorm_decoder_step'


def layer_norm(x, g=None, b=None, eps=1e-5):
    xf = x.astype(jnp.float32)
    mu = xf.mean(-1, keepdims=True)
    var = jnp.mean(jnp.square(xf - mu), -1, keepdims=True)
    y = (xf - mu) * lax.rsqrt(var + eps)
    if g is not None:
        y = y * g + b
    return y.astype(x.dtype)


def rms_norm(x, g, eps=1e-6):
    xf = x.astype(jnp.float32)
    y = xf * lax.rsqrt(jnp.mean(jnp.square(xf), -1, keepdims=True) + eps) * g
    return y.astype(x.dtype)


def apply_rope(x, pos, n_rot, theta):
    half = n_rot // 2
    inv = theta ** (-jnp.arange(half, dtype=jnp.float32) / half)
    ang = pos.astype(jnp.float32)[:, None] * inv[None, :]
    shape = (1, pos.shape[0]) + (1,) * (x.ndim - 3) + (half,)
    cos, sin = jnp.cos(ang).reshape(shape), jnp.sin(ang).reshape(shape)
    xf = x[..., :n_rot].astype(jnp.float32)
    x1, x2 = xf[..., :half], xf[..., half:]
    rot = jnp.concatenate([x1 * cos - x2 * sin, x2 * cos + x1 * sin], -1).astype(x.dtype)
    return jnp.concatenate([rot, x[..., n_rot:]], -1)


def masked_softmax(s, mask):
    s = jnp.where(mask, s.astype(jnp.float32), NEG_INF)
    return jnp.where(mask, jax.nn.softmax(s, axis=-1), 0.0)


def nsa_compress(rows, pos_emb, w1, w2):
    B, L, dh = rows.shape
    r = CMP_BLOCK // CMP_STRIDE
    n_chunk = L // CMP_STRIDE
    nc = n_chunk - r + 1
    chunks = rows[:, :n_chunk * CMP_STRIDE].reshape(B, n_chunk, CMP_STRIDE, dh)
    part = jnp.einsum('bncd,kcdf->bnkf', chunks, w1.reshape(r, CMP_STRIDE, dh, -1))
    pre = sum(part[:, k:k + nc, k] for k in range(r)) + pos_emb.reshape(-1) @ w1
    return jax.nn.gelu(pre) @ w2


def compress_kv(rows, p):
    kc = nsa_compress(rows[..., :DH_A], p['nsa_cmp_pos_k'], p['nsa_cmp_w1_k'], p['nsa_cmp_w2_k'])
    vc = nsa_compress(rows[..., DH_A:], p['nsa_cmp_pos_v'], p['nsa_cmp_w1_v'], p['nsa_cmp_w2_v'])
    return kc, vc


def nsa_core(q, g, t_pos, kc, vc, gather_sel, kw, vw, kw_pos, n_sel_blk):
    B, T, H, dh = q.shape
    scale = dh ** -0.5
    nc = kc.shape[1]
    blk_last = jnp.arange(nc) * CMP_STRIDE + (CMP_BLOCK - 1)
    m_c = (blk_last[None, :] <= t_pos[:, None])[None, :, None, :]
    p_c = masked_softmax(jnp.einsum('bthd,bnd->bthn', q, kc) * scale, m_c)
    o_c = jnp.einsum('bthn,bnd->bthd', p_c.astype(vc.dtype), vc)
    ci = jnp.arange(nc)[:, None] * CMP_STRIDE
    sj = jnp.arange(n_sel_blk)[None, :] * SEL_BLOCK
    cover = ((ci < sj + SEL_BLOCK) & (ci + CMP_BLOCK > sj)).astype(jnp.float32)
    imp = p_c.sum(axis=2) @ cover
    j = jnp.arange(n_sel_blk)[None, :]
    cur = (t_pos // SEL_BLOCK)[:, None]
    forced = (j == 0) | (j == cur) | (j == cur - 1)
    score = jnp.where(j <= cur, imp + jnp.where(forced, FORCE_BONUS, 0.0), NEG_INF)
    k_sel = min(N_SEL, n_sel_blk)
    _, idx = lax.top_k(score, k_sel)
    blk_ok = idx <= cur[None]
    sel_pos = idx[..., None] * SEL_BLOCK + jnp.arange(SEL_BLOCK)
    m_s = (blk_ok[..., None] & (sel_pos <= t_pos[None, :, None, None])).reshape(B, T, 1, k_sel * SEL_BLOCK)
    ks, vs = gather_sel(sel_pos.reshape(B, T, k_sel * SEL_BLOCK))
    p_s = masked_softmax(jnp.einsum('bthd,btpd->bthp', q, ks) * scale, m_s)
    o_s = jnp.einsum('bthp,btpd->bthd', p_s.astype(vs.dtype), vs)
    dist = t_pos[:, None] - kw_pos[None, :]
    m_w = ((dist >= 0) & (dist < WINDOW) & (kw_pos[None, :] >= 0))[None, :, None, :]
    p_w = masked_softmax(jnp.einsum('bthd,bld->bthl', q, kw) * scale, m_w)
    o_w = jnp.einsum('bthl,bld->bthd', p_w.astype(vw.dtype), vw)
    o = g[..., 0:1] * o_c + g[..., 1:2] * o_s + g[..., 2:3] * o_w
    return o.reshape(B, T, H * dh)


def mla_core(q_lat, q_rope, t_pos, kv_lat, k_rope, kv_pos):
    s = (jnp.einsum('bthr,blr->bhtl', q_lat, kv_lat) + jnp.einsum('bthd,bld->bhtl', q_rope, k_rope)) * MLA_SCALE
    mask = (kv_pos[None, :] <= t_pos[:, None])[None, None]
    p = masked_softmax(s, mask)
    return jnp.einsum('bhtl,blr->bthr', p.astype(kv_lat.dtype), kv_lat)


def mixer_inputs(h, pos, p):
    B, T, _ = h.shape
    q_a, kv_a, g_a, qd_b, kvd_b, kr_b, g_m = jnp.split(h @ p['w_in'], IN_SPLITS, axis=-1)
    q_a = apply_rope(q_a.reshape(B, T, H_A, DH_A), pos, ROT_A, THETA_A)
    k_cmp, v_cmp, k_sel, v_sel, k_win, v_win = jnp.split(kv_a, 6, axis=-1)
    rope_a = lambda k: apply_rope(k, pos, ROT_A, THETA_A)
    q_b = (rms_norm(qd_b, p['mla_q_norm']) @ p['mla_w_uq']).reshape(B, T, H_B, NOPE_B + ROPE_B)
    q_lat = jnp.einsum('bthn,rhn->bthr', q_b[..., :NOPE_B], p['mla_w_uk'])
    q_rope = apply_rope(q_b[..., NOPE_B:], pos, ROPE_B, THETA_B)
    c_kv = rms_norm(kvd_b, p['mla_kv_norm'])
    k_rope = apply_rope(kr_b, pos, ROPE_B, THETA_B)
    return dict(
        q_a=q_a,
        g_a=jax.nn.sigmoid(g_a).reshape(B, T, H_A, 3),
        cmp_rows=jnp.concatenate([rope_a(k_cmp), v_cmp], -1),
        sel_rows=jnp.concatenate([rope_a(k_sel), v_sel], -1),
        win_rows=jnp.concatenate([rope_a(k_win), v_win], -1),
        q_lat=q_lat,
        q_rope=q_rope,
        mla_rows=jnp.concatenate([c_kv, k_rope], -1),
        g_m=jax.nn.sigmoid(g_m).reshape(B, T, 2, D_MODEL))


def mixer_output(o_a, o_lat, g_m, p):
    B, T = o_a.shape[:2]
    o_b = jnp.einsum('bthr,rhv->bthv', o_lat, p['mla_w_uv']).reshape(B, T, H_B * DV_B)
    y = g_m[:, :, 0] * (o_a @ p['w_proj_a']) + g_m[:, :, 1] * (o_b @ p['w_proj_b'])
    return y @ p['w_out']


def mix_prompt(h, p):
    B, S, _ = h.shape
    pos = jnp.arange(S)
    m = mixer_inputs(h, pos, p)
    kc, vc = compress_kv(m['cmp_rows'], p)
    n_sel_blk = -(-S // SEL_BLOCK)
    nq = S // Q_BLOCK
    win_pad = jnp.pad(m['win_rows'], ((0, 0), (WINDOW, 0), (0, 0)))
    sel_rows = m['sel_rows']
    b_idx = jnp.arange(B)[:, None, None]

    def gather_sel(sel_pos):
        rows = sel_rows[b_idx, jnp.clip(sel_pos, 0, S - 1)]
        return rows[..., :DH_A], rows[..., DH_A:]

    def nsa_block(args):
        q_blk, g_blk, bi = args
        t_pos = bi * Q_BLOCK + jnp.arange(Q_BLOCK)
        w = lax.dynamic_slice_in_dim(win_pad, bi * Q_BLOCK, Q_BLOCK + WINDOW, axis=1)
        kw_pos = bi * Q_BLOCK - WINDOW + jnp.arange(Q_BLOCK + WINDOW)
        return nsa_core(q_blk, g_blk, t_pos, kc, vc, gather_sel, w[..., :DH_A], w[..., DH_A:], kw_pos, n_sel_blk)

    kv_lat, k_rope = m['mla_rows'][..., :KV_RANK], m['mla_rows'][..., KV_RANK:]

    def mla_block(args):
        ql, qr, bi = args
        return mla_core(ql, qr, bi * Q_BLOCK + jnp.arange(Q_BLOCK), kv_lat, k_rope, pos)

    to_blocks = lambda a: a.reshape((B, nq, Q_BLOCK) + a.shape[2:]).swapaxes(0, 1)
    from_blocks = lambda a: a.swapaxes(0, 1).reshape((B, S) + a.shape[3:])
    o_a = from_blocks(lax.map(nsa_block, (to_blocks(m['q_a']), to_blocks(m['g_a']), jnp.arange(nq))))
    o_lat = from_blocks(lax.map(mla_block, (to_blocks(m['q_lat']), to_blocks(m['q_rope']), jnp.arange(nq))))
    new_win = m['win_rows'][:, S - min(WINDOW, S):]
    return mixer_output(o_a, o_lat, m['g_m'], p), (m['cmp_rows'], m['sel_rows'], m['mla_rows'], new_win)


def mix_sample(h, p, l, cache_nsa_cmp, cache_nsa_sel, cache_mla, state_nsa_win, page_table):
    B, T, _ = h.shape
    pos = PAST_LEN + jnp.arange(T)
    m = mixer_inputs(h, pos, p)
    past_cmp = cache_nsa_cmp[l, page_table].reshape(B, PAST_LEN, 2 * DH_A)
    kc, vc = compress_kv(jnp.concatenate([past_cmp, m['cmp_rows']], 1), p)
    b_idx = jnp.arange(B)[:, None, None]

    def gather_sel(sel_pos):
        in_past = sel_pos < PAST_LEN
        pp = jnp.clip(sel_pos, 0, PAST_LEN - 1)
        phys = page_table[b_idx, pp // PAGE_SIZE]
        rows_past = cache_nsa_sel[l, phys, pp % PAGE_SIZE]
        rows_new = m['sel_rows'][b_idx, jnp.clip(sel_pos - PAST_LEN, 0, T - 1)]
        rows = jnp.where(in_past[..., None], rows_past, rows_new)
        return rows[..., :DH_A], rows[..., DH_A:]

    win_len = state_nsa_win.shape[2]
    win = jnp.concatenate([state_nsa_win[l], m['win_rows']], 1)
    kw_pos = PAST_LEN - win_len + jnp.arange(win_len + T)
    n_sel_blk = -(-(PAST_LEN + T) // SEL_BLOCK)
    o_a = nsa_core(m['q_a'], m['g_a'], pos, kc, vc, gather_sel, win[..., :DH_A], win[..., DH_A:], kw_pos, n_sel_blk)
    past_mla = cache_mla[l, page_table].reshape(B, PAST_LEN, KV_RANK + ROPE_B)
    rows_mla = jnp.concatenate([past_mla, m['mla_rows']], 1)
    o_lat = mla_core(m['q_lat'], m['q_rope'], pos, rows_mla[..., :KV_RANK], rows_mla[..., KV_RANK:], jnp.arange(PAST_LEN + T))
    new_win = win[:, T:]
    return mixer_output(o_a, o_lat, m['g_m'], p), (m['cmp_rows'], m['sel_rows'], m['mla_rows'], new_win)


def moe_ffn(h, p):
    N = h.shape[0]
    lg = (h @ p['moe_w_group']).astype(jnp.float32) + p['moe_b_group']
    g_sel = jnp.argmax(lg, axis=-1)
    pg_top = jnp.take_along_axis(jax.nn.softmax(lg, axis=-1), g_sel[:, None], axis=1)
    le = ((h @ p['moe_w_expert']).astype(jnp.float32) + p['moe_b_expert']).reshape(N, N_GROUPS, EXPERTS_PER_GROUP)
    le_g = jnp.take_along_axis(le, g_sel[:, None, None], axis=1)[:, 0]
    v2, j2 = lax.top_k(jax.nn.softmax(le_g, axis=-1), TOP_K)
    w = pg_top * v2 / v2.sum(-1, keepdims=True)
    e_idx = g_sel[:, None] * EXPERTS_PER_GROUP + j2
    A = N * TOP_K
    e_flat, w_flat = e_idx.reshape(A), w.reshape(A)
    tok = jnp.repeat(jnp.arange(N, dtype=jnp.int32), TOP_K)
    order = jnp.argsort(e_flat, stable=True)
    e_s = e_flat[order]
    counts = jnp.zeros((N_EXPERTS,), jnp.int32).at[e_flat].add(1)
    padded = (counts + MOE_BLOCK - 1) // MOE_BLOCK * MOE_BLOCK
    start = jnp.cumsum(counts) - counts
    pad_end = jnp.cumsum(padded)
    pad_start = pad_end - padded
    dest = pad_start[e_s] + jnp.arange(A) - start[e_s]
    n_blk = (A + N_EXPERTS * (MOE_BLOCK - 1)) // MOE_BLOCK
    R = n_blk * MOE_BLOCK
    row_tok = jnp.zeros((R,), jnp.int32).at[dest].set(tok[order])
    row_w = jnp.zeros((R,), jnp.float32).at[dest].set(w_flat[order])
    blk_e = jnp.minimum(jnp.searchsorted(pad_end, jnp.arange(n_blk) * MOE_BLOCK, side='right'), N_EXPERTS - 1)

    def expert_block(args):
        e, toks, wts = args
        xb = h[toks]
        a = xb @ p['moe_w_gate'][e]
        u = xb @ p['moe_w_up'][e]
        y = (jax.nn.silu(a) * u) @ p['moe_w_down'][e]
        return y * wts[:, None].astype(y.dtype)

    y = lax.map(expert_block, (blk_e, row_tok.reshape(n_blk, MOE_BLOCK), row_w.reshape(n_blk, MOE_BLOCK)))
    return jax.ops.segment_sum(y.reshape(R, -1), row_tok, num_segments=N)


def trunk_layer(x, c, p, token_mixer):
    B, T, D = x.shape
    ada = (jax.nn.silu(c) @ p['w_ada'] + p['b_ada']).reshape(B, 6, 1, D)
    h = layer_norm(x) * (1.0 + ada[:, 1]) + ada[:, 0]
    mix, rows = token_mixer(h)
    x = layer_norm(ALPHA * x + ada[:, 2] * mix, p['ln1_g'], p['ln1_b'])
    h = layer_norm(x) * (1.0 + ada[:, 4]) + ada[:, 3]
    ff = moe_ffn(h.reshape(B * T, D), p).reshape(B, T, D)
    x = layer_norm(ALPHA * x + ada[:, 5] * ff, p['ln2_g'], p['ln2_b'])
    return x, rows


def stack_layers(rows, i):
    return jnp.stack([r[i] for r in rows])


def setup_inputs(seed: int = 0) -> dict:
    key = jax.random.key(seed)
    ks = iter(jax.random.split(key, 48))
    nrm = lambda shape, scale: jax.random.normal(next(ks), shape, jnp.float32) * scale
    n_pages = PAST_LEN // PAGE_SIZE
    n_used = DEC_BATCH * n_pages
    n_pool = n_used + max(1, n_used // 4)
    win_buf = min(WINDOW, PAST_LEN)
    page_table = jax.random.permutation(next(ks), n_pool)[:n_used].reshape(DEC_BATCH, n_pages).astype(jnp.int32)
    return {
        'x_prompt': nrm((BATCH, SEQ, D_MODEL), 1.0),
        'x_sample': nrm((DEC_BATCH, DEC_SEQ, D_MODEL), 1.0),
        'cache_nsa_cmp': nrm((DEPTH, n_pool, PAGE_SIZE, 2 * DH_A), 1.0),
        'cache_nsa_sel': nrm((DEPTH, n_pool, PAGE_SIZE, 2 * DH_A), 1.0),
        'cache_mla': nrm((DEPTH, n_pool, PAGE_SIZE, KV_RANK + ROPE_B), 1.0),
        'state_nsa_win': nrm((DEPTH, DEC_BATCH, win_buf, 2 * DH_A), 1.0),
        'page_table': page_table,
        'c_prompt': nrm((BATCH, D_MODEL), 1.0),
        'c_sample': nrm((DEC_BATCH, D_MODEL), 1.0),
        'w_ada': nrm((DEPTH, D_MODEL, 6 * D_MODEL), 0.5 * D_MODEL ** -0.5),
        'b_ada': nrm((DEPTH, 6 * D_MODEL), 0.02),
        'w_in': nrm((DEPTH, D_MODEL, IN_COLS), D_MODEL ** -0.5),
        'nsa_cmp_pos_k': nrm((DEPTH, CMP_BLOCK, DH_A), 0.1),
        'nsa_cmp_w1_k': nrm((DEPTH, CMP_BLOCK * DH_A, CMP_HID), (CMP_BLOCK * DH_A) ** -0.5),
        'nsa_cmp_w2_k': nrm((DEPTH, CMP_HID, DH_A), CMP_HID ** -0.5),
        'nsa_cmp_pos_v': nrm((DEPTH, CMP_BLOCK, DH_A), 0.1),
        'nsa_cmp_w1_v': nrm((DEPTH, CMP_BLOCK * DH_A, CMP_HID), (CMP_BLOCK * DH_A) ** -0.5),
        'nsa_cmp_w2_v': nrm((DEPTH, CMP_HID, DH_A), CMP_HID ** -0.5),
        'mla_q_norm': 1.0 + nrm((DEPTH, Q_LORA), 0.01),
        'mla_w_uq': nrm((DEPTH, Q_LORA, H_B * (NOPE_B + ROPE_B)), Q_LORA ** -0.5),
        'mla_kv_norm': 1.0 + nrm((DEPTH, KV_RANK), 0.01),
        'mla_w_uk': nrm((DEPTH, KV_RANK, H_B, NOPE_B), KV_RANK ** -0.5),
        'mla_w_uv': nrm((DEPTH, KV_RANK, H_B, DV_B), KV_RANK ** -0.5),
        'w_proj_a': nrm((DEPTH, H_A * DH_A, D_MODEL), (H_A * DH_A) ** -0.5),
        'w_proj_b': nrm((DEPTH, H_B * DV_B, D_MODEL), (H_B * DV_B) ** -0.5),
        'w_out': nrm((DEPTH, D_MODEL, D_MODEL), BETA * D_MODEL ** -0.5),
        'ln1_g': 1.0 + nrm((DEPTH, D_MODEL), 0.01),
        'ln1_b': nrm((DEPTH, D_MODEL), 0.01),
        'moe_w_group': nrm((DEPTH, D_MODEL, N_GROUPS), D_MODEL ** -0.5),
        'moe_b_group': nrm((DEPTH, N_GROUPS), 0.01),
        'moe_w_expert': nrm((DEPTH, D_MODEL, N_EXPERTS), D_MODEL ** -0.5),
        'moe_b_expert': nrm((DEPTH, N_EXPERTS), 0.01),
        'moe_w_gate': nrm((DEPTH, N_EXPERTS, D_MODEL, D_FF), D_MODEL ** -0.5),
        'moe_w_up': nrm((DEPTH, N_EXPERTS, D_MODEL, D_FF), D_MODEL ** -0.5),
        'moe_w_down': nrm((DEPTH, N_EXPERTS, D_FF, D_MODEL), BETA * D_FF ** -0.5),
        'ln2_g': 1.0 + nrm((DEPTH, D_MODEL), 0.01),
        'ln2_b': nrm((DEPTH, D_MODEL), 0.01),
    }


def reference(x_prompt, x_sample, cache_nsa_cmp, cache_nsa_sel, cache_mla, state_nsa_win, page_table, c_prompt, c_sample, w_ada, b_ada, w_in, nsa_cmp_pos_k, nsa_cmp_w1_k, nsa_cmp_w2_k, nsa_cmp_pos_v, nsa_cmp_w1_v, nsa_cmp_w2_v, mla_q_norm, mla_w_uq, mla_kv_norm, mla_w_uk, mla_w_uv, w_proj_a, w_proj_b, w_out, ln1_g, ln1_b, moe_w_group, moe_b_group, moe_w_expert, moe_b_expert, moe_w_gate, moe_w_up, moe_w_down, ln2_g, ln2_b):
    xp, xs = x_prompt, x_sample
    rows_p, rows_s = [], []
    for l in range(DEPTH):
        p = dict(w_ada=w_ada[l], b_ada=b_ada[l], w_in=w_in[l],
                 nsa_cmp_pos_k=nsa_cmp_pos_k[l], nsa_cmp_w1_k=nsa_cmp_w1_k[l], nsa_cmp_w2_k=nsa_cmp_w2_k[l],
                 nsa_cmp_pos_v=nsa_cmp_pos_v[l], nsa_cmp_w1_v=nsa_cmp_w1_v[l], nsa_cmp_w2_v=nsa_cmp_w2_v[l],
                 mla_q_norm=mla_q_norm[l], mla_w_uq=mla_w_uq[l], mla_kv_norm=mla_kv_norm[l],
                 mla_w_uk=mla_w_uk[l], mla_w_uv=mla_w_uv[l],
                 w_proj_a=w_proj_a[l], w_proj_b=w_proj_b[l], w_out=w_out[l],
                 ln1_g=ln1_g[l], ln1_b=ln1_b[l],
                 moe_w_group=moe_w_group[l], moe_b_group=moe_b_group[l],
                 moe_w_expert=moe_w_expert[l], moe_b_expert=moe_b_expert[l],
                 moe_w_gate=moe_w_gate[l], moe_w_up=moe_w_up[l], moe_w_down=moe_w_down[l],
                 ln2_g=ln2_g[l], ln2_b=ln2_b[l])
        xp, rp = trunk_layer(xp, c_prompt, p, lambda h: mix_prompt(h, p))
        xs, rs = trunk_layer(xs, c_sample, p, lambda h: mix_sample(h, p, l, cache_nsa_cmp, cache_nsa_sel, cache_mla, state_nsa_win, page_table))
        rows_p.append(rp)
        rows_s.append(rs)
    return (xp, xs,
            stack_layers(rows_p, 0), stack_layers(rows_s, 0),
            stack_layers(rows_p, 1), stack_layers(rows_s, 1),
            stack_layers(rows_p, 2), stack_layers(rows_s, 2),
            stack_layers(rows_p, 3), stack_layers(rows_s, 3))
```

```python
import functools

import jax
import jax.numpy as jnp
import numpy as np
from jax import lax
from jax.experimental import pallas as pl
from jax.experimental.pallas import tpu as pltpu

D_MODEL = 2048
BATCH = 4
SEQ = 4096
DEPTH = 2
DEC_BATCH = 128
DEC_SEQ = 1
PAST_LEN = 16384
PAGE_SIZE = 128

H_A = D_MODEL // 128
DH_A = 64
ROT_A = DH_A // 4
THETA_A = 500000.0
CMP_BLOCK = 32
CMP_STRIDE = 16
CMP_HID = 2 * DH_A
SEL_BLOCK = 64
N_SEL = 16
WINDOW = 512
H_B = D_MODEL // 256
NOPE_B = 128
ROPE_B = 64
DV_B = 128
Q_LORA = D_MODEL // 4
KV_RANK = D_MODEL // 16
THETA_B = 10000.0
MLA_SCALE = (NOPE_B + ROPE_B) ** -0.5
N_GROUPS = 4
EXPERTS_PER_GROUP = 8
N_EXPERTS = N_GROUPS * EXPERTS_PER_GROUP
TOP_K = 2
D_FF = D_MODEL // 4
MOE_BLOCK = 128
Q_BLOCK = 128
ALPHA = (2.0 * DEPTH) ** 0.25
NEG_INF = -1e30
FORCE_BONUS = 1e6
IN_SIZES = (H_A * DH_A, 6 * DH_A, 3 * H_A, Q_LORA, KV_RANK, ROPE_B, 2 * D_MODEL)
IN_SPLITS = [int(v) for v in np.cumsum(IN_SIZES)[:-1]]

VMEM_LIMIT_BYTES = 48 * 1024 * 1024


def _matmul_kernel(x_ref, w_ref, o_ref):
    o_ref[...] = jnp.dot(x_ref[...].astype(jnp.bfloat16), w_ref[...].astype(jnp.bfloat16),
                         preferred_element_type=jnp.float32)


def matmul(x, w, tm=512, tn=512):
    M, K = x.shape
    N = w.shape[1]
    tm = min(tm, M)
    tn = min(tn, N)
    return pl.pallas_call(
        _matmul_kernel,
        grid=(pl.cdiv(M, tm), pl.cdiv(N, tn)),
        in_specs=[pl.BlockSpec((tm, K), lambda i, j: (i, 0)),
                  pl.BlockSpec((K, tn), lambda i, j: (0, j))],
        out_specs=pl.BlockSpec((tm, tn), lambda i, j: (i, j)),
        out_shape=jax.ShapeDtypeStruct((M, N), jnp.float32),
        compiler_params=pltpu.CompilerParams(
            dimension_semantics=("arbitrary", "arbitrary"), vmem_limit_bytes=VMEM_LIMIT_BYTES),
        name="matmul",
    )(x, w)


def mm(x, w):
    lead = x.shape[:-1]
    return matmul(x.reshape(-1, x.shape[-1]), w).reshape(lead + (w.shape[1],))


def layer_norm(x, g=None, b=None, eps=1e-5):
    mu = x.mean(-1, keepdims=True)
    var = jnp.mean(jnp.square(x - mu), -1, keepdims=True)
    y = (x - mu) * lax.rsqrt(var + eps)
    if g is not None:
        y = y * g + b
    return y


def rms_norm(x, g, eps=1e-6):
    return x * lax.rsqrt(jnp.mean(jnp.square(x), -1, keepdims=True) + eps) * g


def apply_rope(x, pos, n_rot, theta):
    half = n_rot // 2
    inv = theta ** (-jnp.arange(half, dtype=jnp.float32) / half)
    ang = pos.astype(jnp.float32)[:, None] * inv[None, :]
    shape = (1, pos.shape[0]) + (1,) * (x.ndim - 3) + (half,)
    cos, sin = jnp.cos(ang).reshape(shape), jnp.sin(ang).reshape(shape)
    xf = x[..., :n_rot]
    x1, x2 = xf[..., :half], xf[..., half:]
    rot = jnp.concatenate([x1 * cos - x2 * sin, x2 * cos + x1 * sin], -1)
    return jnp.concatenate([rot, x[..., n_rot:]], -1)


def masked_softmax(s, mask):
    s = jnp.where(mask, s, NEG_INF)
    return jnp.where(mask, jax.nn.softmax(s, axis=-1), 0.0)


def nsa_compress(rows, pos_emb, w1, w2):
    B, L, dh = rows.shape
    r = CMP_BLOCK // CMP_STRIDE
    n_chunk = L // CMP_STRIDE
    nc = n_chunk - r + 1
    chunks = rows[:, :n_chunk * CMP_STRIDE].reshape(B, n_chunk, CMP_STRIDE, dh)
    part = jnp.einsum('bncd,kcdf->bnkf', chunks, w1.reshape(r, CMP_STRIDE, dh, -1))
    pre = sum(part[:, k:k + nc, k] for k in range(r)) + pos_emb.reshape(-1) @ w1
    return jax.nn.gelu(pre) @ w2


def compress_kv(rows, p):
    kc = nsa_compress(rows[..., :DH_A], p['nsa_cmp_pos_k'], p['nsa_cmp_w1_k'], p['nsa_cmp_w2_k'])
    vc = nsa_compress(rows[..., DH_A:], p['nsa_cmp_pos_v'], p['nsa_cmp_w1_v'], p['nsa_cmp_w2_v'])
    return kc, vc


def nsa_core(q, g, t_pos, kc, vc, gather_sel, kw, vw, kw_pos, n_sel_blk):
    B, T, H, dh = q.shape
    scale = dh ** -0.5
    nc = kc.shape[1]
    blk_last = jnp.arange(nc) * CMP_STRIDE + (CMP_BLOCK - 1)
    m_c = (blk_last[None, :] <= t_pos[:, None])[None, :, None, :]
    p_c = masked_softmax(jnp.einsum('bthd,bnd->bthn', q, kc) * scale, m_c)
    o_c = jnp.einsum('bthn,bnd->bthd', p_c, vc)
    ci = jnp.arange(nc)[:, None] * CMP_STRIDE
    sj = jnp.arange(n_sel_blk)[None, :] * SEL_BLOCK
    cover = ((ci < sj + SEL_BLOCK) & (ci + CMP_BLOCK > sj)).astype(jnp.float32)
    imp = p_c.sum(axis=2) @ cover
    j = jnp.arange(n_sel_blk)[None, :]
    cur = (t_pos // SEL_BLOCK)[:, None]
    forced = (j == 0) | (j == cur) | (j == cur - 1)
    score = jnp.where(j <= cur, imp + jnp.where(forced, FORCE_BONUS, 0.0), NEG_INF)
    k_sel = min(N_SEL, n_sel_blk)
    _, idx = lax.top_k(score, k_sel)
    blk_ok = idx <= cur[None]
    sel_pos = idx[..., None] * SEL_BLOCK + jnp.arange(SEL_BLOCK)
    m_s = (blk_ok[..., None] & (sel_pos <= t_pos[None, :, None, None])).reshape(B, T, 1, k_sel * SEL_BLOCK)
    ks, vs = gather_sel(sel_pos.reshape(B, T, k_sel * SEL_BLOCK))
    p_s = masked_softmax(jnp.einsum('bthd,btpd->bthp', q, ks) * scale, m_s)
    o_s = jnp.einsum('bthp,btpd->bthd', p_s, vs)
    dist = t_pos[:, None] - kw_pos[None, :]
    m_w = ((dist >= 0) & (dist < WINDOW) & (kw_pos[None, :] >= 0))[None, :, None, :]
    p_w = masked_softmax(jnp.einsum('bthd,bld->bthl', q, kw) * scale, m_w)
    o_w = jnp.einsum('bthl,bld->bthd', p_w, vw)
    o = g[..., 0:1] * o_c + g[..., 1:2] * o_s + g[..., 2:3] * o_w
    return o.reshape(B, T, H * dh)


def mla_core(q_lat, q_rope, t_pos, kv_lat, k_rope, kv_pos):
    s = (jnp.einsum('bthr,blr->bhtl', q_lat, kv_lat) + jnp.einsum('bthd,bld->bhtl', q_rope, k_rope)) * MLA_SCALE
    mask = (kv_pos[None, :] <= t_pos[:, None])[None, None]
    p = masked_softmax(s, mask)
    return jnp.einsum('bhtl,blr->bthr', p, kv_lat)


def mixer_inputs(h, pos, p):
    B, T, _ = h.shape
    q_a, kv_a, g_a, qd_b, kvd_b, kr_b, g_m = jnp.split(mm(h, p['w_in']), IN_SPLITS, axis=-1)
    q_a = apply_rope(q_a.reshape(B, T, H_A, DH_A), pos, ROT_A, THETA_A)
    k_cmp, v_cmp, k_sel, v_sel, k_win, v_win = jnp.split(kv_a, 6, axis=-1)
    rope_a = lambda k: apply_rope(k, pos, ROT_A, THETA_A)
    q_b = mm(rms_norm(qd_b, p['mla_q_norm']), p['mla_w_uq']).reshape(B, T, H_B, NOPE_B + ROPE_B)
    q_lat = jnp.einsum('bthn,rhn->bthr', q_b[..., :NOPE_B], p['mla_w_uk'])
    q_rope = apply_rope(q_b[..., NOPE_B:], pos, ROPE_B, THETA_B)
    c_kv = rms_norm(kvd_b, p['mla_kv_norm'])
    k_rope = apply_rope(kr_b, pos, ROPE_B, THETA_B)
    return dict(
        q_a=q_a,
        g_a=jax.nn.sigmoid(g_a).reshape(B, T, H_A, 3),
        cmp_rows=jnp.concatenate([rope_a(k_cmp), v_cmp], -1),
        sel_rows=jnp.concatenate([rope_a(k_sel), v_sel], -1),
        win_rows=jnp.concatenate([rope_a(k_win), v_win], -1),
        q_lat=q_lat,
        q_rope=q_rope,
        mla_rows=jnp.concatenate([c_kv, k_rope], -1),
        g_m=jax.nn.sigmoid(g_m).reshape(B, T, 2, D_MODEL))


def mixer_output(o_a, o_lat, g_m, p):
    B, T = o_a.shape[:2]
    o_b = jnp.einsum('bthr,rhv->bthv', o_lat, p['mla_w_uv']).reshape(B, T, H_B * DV_B)
    y = g_m[:, :, 0] * mm(o_a, p['w_proj_a']) + g_m[:, :, 1] * mm(o_b, p['w_proj_b'])
    return mm(y, p['w_out'])


def mix_prompt(h, p):
    B, S, _ = h.shape
    pos = jnp.arange(S)
    m = mixer_inputs(h, pos, p)
    kc, vc = compress_kv(m['cmp_rows'], p)
    n_sel_blk = -(-S // SEL_BLOCK)
    nq = S // Q_BLOCK
    win_pad = jnp.pad(m['win_rows'], ((0, 0), (WINDOW, 0), (0, 0)))
    sel_rows = m['sel_rows']
    b_idx = jnp.arange(B)[:, None, None]

    def gather_sel(sel_pos):
        rows = sel_rows[b_idx, jnp.clip(sel_pos, 0, S - 1)]
        return rows[..., :DH_A], rows[..., DH_A:]

    def nsa_block(args):
        q_blk, g_blk, bi = args
        t_pos = bi * Q_BLOCK + jnp.arange(Q_BLOCK)
        w = lax.dynamic_slice_in_dim(win_pad, bi * Q_BLOCK, Q_BLOCK + WINDOW, axis=1)
        kw_pos = bi * Q_BLOCK - WINDOW + jnp.arange(Q_BLOCK + WINDOW)
        return nsa_core(q_blk, g_blk, t_pos, kc, vc, gather_sel, w[..., :DH_A], w[..., DH_A:], kw_pos, n_sel_blk)

    kv_lat, k_rope = m['mla_rows'][..., :KV_RANK], m['mla_rows'][..., KV_RANK:]

    def mla_block(args):
        ql, qr, bi = args
        return mla_core(ql, qr, bi * Q_BLOCK + jnp.arange(Q_BLOCK), kv_lat, k_rope, pos)

    to_blocks = lambda a: a.reshape((B, nq, Q_BLOCK) + a.shape[2:]).swapaxes(0, 1)
    from_blocks = lambda a: a.swapaxes(0, 1).reshape((B, S) + a.shape[3:])
    o_a = from_blocks(lax.map(nsa_block, (to_blocks(m['q_a']), to_blocks(m['g_a']), jnp.arange(nq))))
    o_lat = from_blocks(lax.map(mla_block, (to_blocks(m['q_lat']), to_blocks(m['q_rope']), jnp.arange(nq))))
    new_win = m['win_rows'][:, S - min(WINDOW, S):]
    return mixer_output(o_a, o_lat, m['g_m'], p), (m['cmp_rows'], m['sel_rows'], m['mla_rows'], new_win)


def mix_sample(h, p, l, cache_nsa_cmp, cache_nsa_sel, cache_mla, state_nsa_win, page_table):
    B, T, _ = h.shape
    pos = PAST_LEN + jnp.arange(T)
    m = mixer_inputs(h, pos, p)
    past_cmp = cache_nsa_cmp[l, page_table].reshape(B, PAST_LEN, 2 * DH_A)
    kc, vc = compress_kv(jnp.concatenate([past_cmp, m['cmp_rows']], 1), p)
    b_idx = jnp.arange(B)[:, None, None]

    def gather_sel(sel_pos):
        in_past = sel_pos < PAST_LEN
        pp = jnp.clip(sel_pos, 0, PAST_LEN - 1)
        phys = page_table[b_idx, pp // PAGE_SIZE]
        rows_past = cache_nsa_sel[l, phys, pp % PAGE_SIZE]
        rows_new = m['sel_rows'][b_idx, jnp.clip(sel_pos - PAST_LEN, 0, T - 1)]
        rows = jnp.where(in_past[..., None], rows_past, rows_new)
        return rows[..., :DH_A], rows[..., DH_A:]

    win_len = state_nsa_win.shape[2]
    win = jnp.concatenate([state_nsa_win[l], m['win_rows']], 1)
    kw_pos = PAST_LEN - win_len + jnp.arange(win_len + T)
    n_sel_blk = -(-(PAST_LEN + T) // SEL_BLOCK)
    o_a = nsa_core(m['q_a'], m['g_a'], pos, kc, vc, gather_sel, win[..., :DH_A], win[..., DH_A:], kw_pos, n_sel_blk)
    past_mla = cache_mla[l, page_table].reshape(B, PAST_LEN, KV_RANK + ROPE_B)
    rows_mla = jnp.concatenate([past_mla, m['mla_rows']], 1)
    o_lat = mla_core(m['q_lat'], m['q_rope'], pos, rows_mla[..., :KV_RANK], rows_mla[..., KV_RANK:], jnp.arange(PAST_LEN + T))
    new_win = win[:, T:]
    return mixer_output(o_a, o_lat, m['g_m'], p), (m['cmp_rows'], m['sel_rows'], m['mla_rows'], new_win)


def moe_ffn(h, p):
    N = h.shape[0]
    lg = (h @ p['moe_w_group']) + p['moe_b_group']
    g_sel = jnp.argmax(lg, axis=-1)
    pg_top = jnp.take_along_axis(jax.nn.softmax(lg, axis=-1), g_sel[:, None], axis=1)
    le = ((h @ p['moe_w_expert']) + p['moe_b_expert']).reshape(N, N_GROUPS, EXPERTS_PER_GROUP)
    le_g = jnp.take_along_axis(le, g_sel[:, None, None], axis=1)[:, 0]
    v2, j2 = lax.top_k(jax.nn.softmax(le_g, axis=-1), TOP_K)
    w = pg_top * v2 / v2.sum(-1, keepdims=True)
    e_idx = g_sel[:, None] * EXPERTS_PER_GROUP + j2
    A = N * TOP_K
    e_flat, w_flat = e_idx.reshape(A), w.reshape(A)
    tok = jnp.repeat(jnp.arange(N, dtype=jnp.int32), TOP_K)
    order = jnp.argsort(e_flat, stable=True)
    e_s = e_flat[order]
    counts = jnp.zeros((N_EXPERTS,), jnp.int32).at[e_flat].add(1)
    padded = (counts + MOE_BLOCK - 1) // MOE_BLOCK * MOE_BLOCK
    start = jnp.cumsum(counts) - counts
    pad_end = jnp.cumsum(padded)
    pad_start = pad_end - padded
    dest = pad_start[e_s] + jnp.arange(A) - start[e_s]
    n_blk = (A + N_EXPERTS * (MOE_BLOCK - 1)) // MOE_BLOCK
    R = n_blk * MOE_BLOCK
    row_tok = jnp.zeros((R,), jnp.int32).at[dest].set(tok[order])
    row_w = jnp.zeros((R,), jnp.float32).at[dest].set(w_flat[order])
    blk_e = jnp.minimum(jnp.searchsorted(pad_end, jnp.arange(n_blk) * MOE_BLOCK, side='right'), N_EXPERTS - 1)

    def expert_block(args):
        e, toks, wts = args
        xb = h[toks]
        a = matmul(xb, p['moe_w_gate'][e])
        u = matmul(xb, p['moe_w_up'][e])
        y = matmul(jax.nn.silu(a) * u, p['moe_w_down'][e])
        return y * wts[:, None]

    y = lax.map(expert_block, (blk_e, row_tok.reshape(n_blk, MOE_BLOCK), row_w.reshape(n_blk, MOE_BLOCK)))
    return jax.ops.segment_sum(y.reshape(R, -1), row_tok, num_segments=N)


def trunk_layer(x, c, p, token_mixer):
    B, T, D = x.shape
    ada = (jax.nn.silu(c) @ p['w_ada'] + p['b_ada']).reshape(B, 6, 1, D)
    h = layer_norm(x) * (1.0 + ada[:, 1]) + ada[:, 0]
    mix, rows = token_mixer(h)
    x = layer_norm(ALPHA * x + ada[:, 2] * mix, p['ln1_g'], p['ln1_b'])
    h = layer_norm(x) * (1.0 + ada[:, 4]) + ada[:, 3]
    ff = moe_ffn(h.reshape(B * T, D), p).reshape(B, T, D)
    x = layer_norm(ALPHA * x + ada[:, 5] * ff, p['ln2_g'], p['ln2_b'])
    return x, rows


PARAM_NAMES = ('w_ada', 'b_ada', 'w_in', 'nsa_cmp_pos_k', 'nsa_cmp_w1_k', 'nsa_cmp_w2_k', 'nsa_cmp_pos_v',
               'nsa_cmp_w1_v', 'nsa_cmp_w2_v', 'mla_q_norm', 'mla_w_uq', 'mla_kv_norm', 'mla_w_uk', 'mla_w_uv',
               'w_proj_a', 'w_proj_b', 'w_out', 'ln1_g', 'ln1_b', 'moe_w_group', 'moe_b_group', 'moe_w_expert',
               'moe_b_expert', 'moe_w_gate', 'moe_w_up', 'moe_w_down', 'ln2_g', 'ln2_b')


def kernel(x_prompt, x_sample, cache_nsa_cmp, cache_nsa_sel, cache_mla, state_nsa_win, page_table, c_prompt, c_sample, w_ada, b_ada, w_in, nsa_cmp_pos_k, nsa_cmp_w1_k, nsa_cmp_w2_k, nsa_cmp_pos_v, nsa_cmp_w1_v, nsa_cmp_w2_v, mla_q_norm, mla_w_uq, mla_kv_norm, mla_w_uk, mla_w_uv, w_proj_a, w_proj_b, w_out, ln1_g, ln1_b, moe_w_group, moe_b_group, moe_w_expert, moe_b_expert, moe_w_gate, moe_w_up, moe_w_down, ln2_g, ln2_b):
    weights = dict(zip(PARAM_NAMES, (w_ada, b_ada, w_in, nsa_cmp_pos_k, nsa_cmp_w1_k, nsa_cmp_w2_k, nsa_cmp_pos_v,
                                     nsa_cmp_w1_v, nsa_cmp_w2_v, mla_q_norm, mla_w_uq, mla_kv_norm, mla_w_uk,
                                     mla_w_uv, w_proj_a, w_proj_b, w_out, ln1_g, ln1_b, moe_w_group, moe_b_group,
                                     moe_w_expert, moe_b_expert, moe_w_gate, moe_w_up, moe_w_down, ln2_g, ln2_b)))
    xp, xs = x_prompt, x_sample
    rows_p, rows_s = [], []
    for l in range(DEPTH):
        p = {k: v[l] for k, v in weights.items()}
        xp, rp = trunk_layer(xp, c_prompt, p, lambda h: mix_prompt(h, p))
        xs, rs = trunk_layer(xs, c_sample, p, lambda h: mix_sample(
            h, p, l, cache_nsa_cmp, cache_nsa_sel, cache_mla, state_nsa_win, page_table))
        rows_p.append(rp)
        rows_s.append(rs)
    stack = lambda rows, i: jnp.stack([r[i] for r in rows])
    return (xp, xs,
            stack(rows_p, 0), stack(rows_s, 0),
            stack(rows_p, 1), stack(rows_s, 1),
            stack(rows_p, 2), stack(rows_s, 2),
            stack(rows_p, 3), stack(rows_s, 3))
```

```python
import functools

import jax
import jax.numpy as jnp
import numpy as np
from jax import lax
from jax.experimental import pallas as pl
from jax.experimental.pallas import tpu as pltpu

D_MODEL = 2048
BATCH = 4
SEQ = 4096
DEPTH = 2
DEC_BATCH = 128
DEC_SEQ = 1
PAST_LEN = 16384
PAGE_SIZE = 128

H_A = D_MODEL // 128
DH_A = 64
ROT_A = DH_A // 4
THETA_A = 500000.0
CMP_BLOCK = 32
CMP_STRIDE = 16
CMP_HID = 2 * DH_A
SEL_BLOCK = 64
N_SEL = 16
WINDOW = 512
H_B = D_MODEL // 256
NOPE_B = 128
ROPE_B = 64
DV_B = 128
Q_LORA = D_MODEL // 4
KV_RANK = D_MODEL // 16
THETA_B = 10000.0
MLA_SCALE = (NOPE_B + ROPE_B) ** -0.5
N_GROUPS = 4
EXPERTS_PER_GROUP = 8
N_EXPERTS = N_GROUPS * EXPERTS_PER_GROUP
TOP_K = 2
D_FF = D_MODEL // 4
MOE_BLOCK = 128
Q_BLOCK = 128
ALPHA = (2.0 * DEPTH) ** 0.25
NEG_INF = -1e30
FORCE_BONUS = 1e6
IN_SIZES = (H_A * DH_A, 6 * DH_A, 3 * H_A, Q_LORA, KV_RANK, ROPE_B, 2 * D_MODEL)
IN_SPLITS = [int(v) for v in np.cumsum(IN_SIZES)[:-1]]

VMEM_LIMIT_BYTES = 52 * 1024 * 1024
LANES = 128
ROWS_KV = 2 * DH_A
KEY_CHUNK = 512
HEADS_PER_STEP = 8
MOE_ROWS_PROMPT = 256
MOE_ROWS_SAMPLE = 64

F32 = jnp.float32
BF16 = jnp.bfloat16


def _cparams(*sem):
    return pltpu.CompilerParams(dimension_semantics=sem, vmem_limit_bytes=VMEM_LIMIT_BYTES)


def _dot_nt(a, b):
    return lax.dot_general(a, b, (((1,), (1,)), ((), ())), preferred_element_type=F32)


def _dot(a, b):
    return jnp.dot(a, b, preferred_element_type=F32)


def _matmul_kernel(x_ref, w_ref, o_ref):
    o_ref[...] = _dot(x_ref[...].astype(BF16), w_ref[...].astype(BF16))


def matmul(x, w, tm=512, tn=512):
    M, K = x.shape
    N = w.shape[1]
    tm = min(tm, M)
    tn = min(tn, N)
    return pl.pallas_call(
        _matmul_kernel,
        grid=(pl.cdiv(M, tm), pl.cdiv(N, tn)),
        in_specs=[pl.BlockSpec((tm, K), lambda i, j: (i, 0)),
                  pl.BlockSpec((K, tn), lambda i, j: (0, j))],
        out_specs=pl.BlockSpec((tm, tn), lambda i, j: (i, j)),
        out_shape=jax.ShapeDtypeStruct((M, N), F32),
        compiler_params=_cparams("arbitrary", "arbitrary"),
        name="matmul",
    )(x, w)


def mm(x, w):
    lead = x.shape[:-1]
    return matmul(x.reshape(-1, x.shape[-1]), w).reshape(lead + (w.shape[1],))


def _compress_kernel(x_ref, w1_ref, posb_ref, w2_ref, o_ref):
    x = x_ref[0].astype(BF16)
    part = _dot(x, w1_ref[...])
    nch = part.shape[0]
    nxt = pltpu.roll(part, nch - 1, 0)
    pre_k = part[:, 0:CMP_HID] + nxt[:, CMP_HID:2 * CMP_HID]
    pre_v = part[:, 2 * CMP_HID:3 * CMP_HID] + nxt[:, 3 * CMP_HID:4 * CMP_HID]
    pre = jnp.concatenate([pre_k, pre_v], axis=1) + posb_ref[...]
    o_ref[0] = _dot(jax.nn.gelu(pre).astype(BF16), w2_ref[...])


def compress_weights(p):
    r = CMP_BLOCK // CMP_STRIDE
    w1k = p['nsa_cmp_w1_k'].reshape(r, CMP_STRIDE, DH_A, CMP_HID)
    w1v = p['nsa_cmp_w1_v'].reshape(r, CMP_STRIDE, DH_A, CMP_HID)
    z = jnp.zeros_like(w1k)
    top = jnp.stack([w1k, z], axis=0)
    bot = jnp.stack([z, w1v], axis=0)
    w = jnp.stack([top, bot], axis=0)
    w1 = w.transpose(3, 0, 4, 1, 2, 5).reshape(CMP_STRIDE * ROWS_KV, 2 * r * CMP_HID).astype(BF16)
    hp = lax.Precision.HIGHEST
    posb = jnp.concatenate([
        jnp.dot(p['nsa_cmp_pos_k'].reshape(1, -1), p['nsa_cmp_w1_k'], precision=hp),
        jnp.dot(p['nsa_cmp_pos_v'].reshape(1, -1), p['nsa_cmp_w1_v'], precision=hp)], axis=1)
    zk = jnp.zeros((CMP_HID, DH_A), F32)
    w2 = jnp.concatenate([jnp.concatenate([p['nsa_cmp_w2_k'], zk], 1),
                          jnp.concatenate([zk, p['nsa_cmp_w2_v']], 1)], 0).astype(BF16)
    return w1, posb, w2


def compress_rows(rows, cw):
    B, L, _ = rows.shape
    nch = L // CMP_STRIDE
    w1, posb, w2 = cw
    x = rows.reshape(B, nch, CMP_STRIDE * ROWS_KV)
    return pl.pallas_call(
        _compress_kernel,
        grid=(B,),
        in_specs=[pl.BlockSpec((1, nch, CMP_STRIDE * ROWS_KV), lambda b: (b, 0, 0)),
                  pl.BlockSpec(w1.shape, lambda b: (0, 0)),
                  pl.BlockSpec(posb.shape, lambda b: (0, 0)),
                  pl.BlockSpec(w2.shape, lambda b: (0, 0))],
        out_specs=pl.BlockSpec((1, nch, ROWS_KV), lambda b: (b, 0, 0)),
        out_shape=jax.ShapeDtypeStruct((B, nch, ROWS_KV), F32),
        compiler_params=_cparams("arbitrary"),
        name="nsa_compress",
    )(x, w1, posb, w2)


def _topk_mask(score, k):
    T, NB = score.shape
    lane = lax.broadcasted_iota(jnp.int32, (T, NB), 1)
    rank = jnp.zeros((T, NB), F32)
    for i in range(NB):
        col = score[:, i:i + 1]
        ahead = (col > score) | ((col == score) & (i < lane))
        rank = rank + jnp.where(ahead, 1.0, 0.0)
    return jnp.where(rank < k, 1.0, 0.0)


def _nsa_cmp_kernel(q_ref, kvc_ref, cover_ref, oc_ref, sel_ref, *, n_heads, n_cmp):
    qi = pl.program_id(1)
    T = q_ref.shape[2]
    q = q_ref[0].reshape(n_heads * T, ROWS_KV)
    kvc = kvc_ref[0].astype(BF16)
    ncp = kvc.shape[0]
    s = _dot_nt(q, kvc).reshape(n_heads, T, ncp)
    t_pos = qi * T + lax.broadcasted_iota(jnp.int32, (T, ncp), 0)
    n_idx = lax.broadcasted_iota(jnp.int32, (T, ncp), 1)
    ok = (n_idx * CMP_STRIDE + (CMP_BLOCK - 1) <= t_pos) & (n_idx < n_cmp)
    s = jnp.where(ok[None], s, NEG_INF)
    m = jnp.max(s, axis=-1, keepdims=True)
    e = jnp.where(ok[None], jnp.exp(s - m), 0.0)
    l = jnp.sum(e, axis=-1, keepdims=True)
    p = e * jnp.where(l > 0.0, 1.0 / l, 0.0)
    oc_ref[0] = _dot(p.reshape(n_heads * T, ncp).astype(BF16), kvc).reshape(n_heads, T, ROWS_KV)
    imp = jnp.dot(jnp.sum(p, axis=0), cover_ref[...], preferred_element_type=F32,
                  precision=lax.Precision.HIGHEST)
    nsb = imp.shape[1]
    j = lax.broadcasted_iota(jnp.int32, (T, nsb), 1)
    cur = (qi * T + lax.broadcasted_iota(jnp.int32, (T, nsb), 0)) // SEL_BLOCK
    forced = (j == 0) | (j == cur) | (j == cur - 1)
    score = jnp.where(j <= cur, imp + jnp.where(forced, FORCE_BONUS, 0.0), NEG_INF)
    sel_ref[0] = _topk_mask(score, min(N_SEL, nsb)) * jnp.where(j <= cur, 1.0, 0.0)


def nsa_cmp_select(qp, kvc, n_cmp):
    B, H, S, _ = qp.shape
    ncp = kvc.shape[1]
    nsb = -(-S // SEL_BLOCK)
    ci = np.arange(ncp)[:, None] * CMP_STRIDE
    sj = np.arange(nsb)[None, :] * SEL_BLOCK
    cover = jnp.asarray(((ci < sj + SEL_BLOCK) & (ci + CMP_BLOCK > sj) & (np.arange(ncp)[:, None] < n_cmp))
                        .astype(np.float32))
    T = Q_BLOCK
    return pl.pallas_call(
        functools.partial(_nsa_cmp_kernel, n_heads=H, n_cmp=n_cmp),
        grid=(B, S // T),
        in_specs=[pl.BlockSpec((1, H, T, ROWS_KV), lambda b, i: (b, 0, i, 0)),
                  pl.BlockSpec((1, ncp, ROWS_KV), lambda b, i: (b, 0, 0)),
                  pl.BlockSpec((ncp, nsb), lambda b, i: (0, 0))],
        out_specs=[pl.BlockSpec((1, H, T, ROWS_KV), lambda b, i: (b, 0, i, 0)),
                   pl.BlockSpec((1, T, nsb), lambda b, i: (b, i, 0))],
        out_shape=[jax.ShapeDtypeStruct((B, H, S, ROWS_KV), F32),
                   jax.ShapeDtypeStruct((B, S, nsb), F32)],
        compiler_params=_cparams("arbitrary", "arbitrary"),
        name="nsa_cmp_select",
    )(qp, kvc, cover)


def _flash_init(m_ref, l_ref, acc_ref):
    m_ref[...] = jnp.full(m_ref.shape, NEG_INF, F32)
    l_ref[...] = jnp.zeros(l_ref.shape, F32)
    acc_ref[...] = jnp.zeros(acc_ref.shape, F32)


def _flash_step(s, bias, v, m_ref, l_ref, acc_ref):
    R, CK = s.shape
    T = bias.shape[0]
    s = (s.reshape(R // T, T, CK) + bias[None]).reshape(R, CK)
    m_prev = m_ref[...]
    m_next = jnp.maximum(m_prev, jnp.max(s, axis=-1, keepdims=True))
    p = jnp.exp(s - jnp.concatenate([m_next] * (CK // LANES), axis=1))
    alpha = jnp.exp(m_prev - m_next)
    l_ref[...] = alpha * l_ref[...] + jnp.sum(p, axis=-1, keepdims=True)
    m_ref[...] = m_next
    acc_ref[...] = acc_ref[...] * alpha + _dot(p.astype(BF16), v)


def _nsa_sel_win_kernel(q_ref, selrows_ref, winrows_ref, sel_ref, os_ref, ow_ref, m_ref, l_ref, acc_ref):
    qi = pl.program_id(1)
    HS, T = q_ref.shape[1], q_ref.shape[2]
    R = HS * T
    q = q_ref[0].reshape(R, ROWS_KV)
    t_row = qi * T + lax.broadcasted_iota(jnp.int32, (T, KEY_CHUNK), 0)
    k_lane = lax.broadcasted_iota(jnp.int32, (T, KEY_CHUNK), 1)
    sel = sel_ref[0].astype(BF16)
    nsb = sel.shape[1]
    blk_row = lax.broadcasted_iota(jnp.int32, (nsb, KEY_CHUNK), 0)
    blk_lane = lax.broadcasted_iota(jnp.int32, (nsb, KEY_CHUNK), 1) // SEL_BLOCK

    _flash_init(m_ref, l_ref, acc_ref)
    n_chunks = (qi * T + T + KEY_CHUNK - 1) // KEY_CHUNK

    def body(c, carry):
        start = pl.multiple_of(c * KEY_CHUNK, KEY_CHUNK)
        kv = selrows_ref[0, pl.ds(start, KEY_CHUNK), :]
        expand = jnp.where(blk_row == blk_lane + c * (KEY_CHUNK // SEL_BLOCK), 1.0, 0.0).astype(BF16)
        picked = _dot(sel, expand)
        bias = jnp.where((picked > 0.5) & (k_lane + c * KEY_CHUNK <= t_row), 0.0, NEG_INF)
        _flash_step(_dot_nt(q, kv), bias, kv, m_ref, l_ref, acc_ref)
        return carry

    lax.fori_loop(0, n_chunks, body, 0)
    os_ref[0] = (acc_ref[...] / l_ref[...]).reshape(HS, T, ROWS_KV)

    WK = WINDOW + T
    start = pl.multiple_of(jnp.maximum(qi * T - WINDOW, 0), T)
    kvw = winrows_ref[0, pl.ds(start, WK), :]
    dist = (qi * T + lax.broadcasted_iota(jnp.int32, (T, WK), 0)) - (start + lax.broadcasted_iota(jnp.int32, (T, WK), 1))
    bias_w = jnp.where((dist >= 0) & (dist < WINDOW), 0.0, NEG_INF)
    s = (_dot_nt(q, kvw).reshape(HS, T, WK) + bias_w[None]).reshape(R, WK)
    e = jnp.exp(s - jnp.max(s, axis=-1, keepdims=True))
    ow = _dot(e.astype(BF16), kvw) / jnp.sum(e, axis=-1, keepdims=True)
    ow_ref[0] = ow.reshape(HS, T, ROWS_KV)


def nsa_sel_win(qp, sel_rows, win_rows, sel):
    B, H, S, _ = qp.shape
    T = Q_BLOCK
    HS = HEADS_PER_STEP
    nsb = sel.shape[2]
    R = HS * T
    out = jax.ShapeDtypeStruct((B, H, S, ROWS_KV), F32)
    return pl.pallas_call(
        _nsa_sel_win_kernel,
        grid=(B, S // T, H // HS),
        in_specs=[pl.BlockSpec((1, HS, T, ROWS_KV), lambda b, i, g: (b, g, i, 0)),
                  pl.BlockSpec((1, S, ROWS_KV), lambda b, i, g: (b, 0, 0)),
                  pl.BlockSpec((1, S, ROWS_KV), lambda b, i, g: (b, 0, 0)),
                  pl.BlockSpec((1, T, nsb), lambda b, i, g: (b, i, 0))],
        out_specs=[pl.BlockSpec((1, HS, T, ROWS_KV), lambda b, i, g: (b, g, i, 0)),
                   pl.BlockSpec((1, HS, T, ROWS_KV), lambda b, i, g: (b, g, i, 0))],
        out_shape=[out, out],
        scratch_shapes=[pltpu.VMEM((R, LANES), F32), pltpu.VMEM((R, LANES), F32), pltpu.VMEM((R, ROWS_KV), F32)],
        compiler_params=_cparams("arbitrary", "arbitrary", "arbitrary"),
        name="nsa_sel_win",
    )(qp, sel_rows, win_rows, sel)


def _mla_kernel(q_ref, rows_ref, o_ref, m_ref, l_ref, acc_ref):
    qi = pl.program_id(1)
    H, T, DK = q_ref.shape[1], q_ref.shape[2], q_ref.shape[3]
    q = q_ref[0].reshape(H * T, DK)
    t_row = qi * T + lax.broadcasted_iota(jnp.int32, (T, KEY_CHUNK), 0)
    k_lane = lax.broadcasted_iota(jnp.int32, (T, KEY_CHUNK), 1)
    _flash_init(m_ref, l_ref, acc_ref)
    n_chunks = (qi * T + T + KEY_CHUNK - 1) // KEY_CHUNK

    def body(c, carry):
        start = pl.multiple_of(c * KEY_CHUNK, KEY_CHUNK)
        rows = rows_ref[0, pl.ds(start, KEY_CHUNK), :]
        bias = jnp.where(k_lane + c * KEY_CHUNK <= t_row, 0.0, NEG_INF)
        _flash_step(_dot_nt(q, rows) * MLA_SCALE, bias, rows[:, :KV_RANK], m_ref, l_ref, acc_ref)
        return carry

    lax.fori_loop(0, n_chunks, body, 0)
    o_ref[0] = (acc_ref[...] / l_ref[...]).reshape(H, T, KV_RANK)


def mla_prompt(qc, rows):
    B, H, S, DK = qc.shape
    T = Q_BLOCK
    R = H * T
    return pl.pallas_call(
        _mla_kernel,
        grid=(B, S // T),
        in_specs=[pl.BlockSpec((1, H, T, DK), lambda b, i: (b, 0, i, 0)),
                  pl.BlockSpec((1, S, DK), lambda b, i: (b, 0, 0))],
        out_specs=pl.BlockSpec((1, H, T, KV_RANK), lambda b, i: (b, 0, i, 0)),
        out_shape=jax.ShapeDtypeStruct((B, H, S, KV_RANK), F32),
        scratch_shapes=[pltpu.VMEM((R, LANES), F32), pltpu.VMEM((R, LANES), F32), pltpu.VMEM((R, KV_RANK), F32)],
        compiler_params=_cparams("arbitrary", "arbitrary"),
        name="mla_prompt",
    )(qc, rows)


def _moe_kernel(blk_e_ref, n_used_ref, x_ref, wg_ref, wu_ref, wd_ref, o_ref):
    i = pl.program_id(0)

    @pl.when(i < n_used_ref[0])
    def _():
        x = x_ref[...]
        a = _dot(x, wg_ref[0])
        u = _dot(x, wu_ref[0])
        o_ref[...] = _dot((jax.nn.silu(a) * u).astype(BF16), wd_ref[0])

    @pl.when(i >= n_used_ref[0])
    def _():
        o_ref[...] = jnp.zeros(o_ref.shape, F32)


def moe_experts(x_sorted, blk_e, n_used, wg, wu, wd, tm):
    R, D = x_sorted.shape
    F = wg.shape[2]
    n_blk = R // tm
    grid_spec = pltpu.PrefetchScalarGridSpec(
        num_scalar_prefetch=2,
        grid=(n_blk,),
        in_specs=[pl.BlockSpec((tm, D), lambda i, be, nu: (i, 0)),
                  pl.BlockSpec((1, D, F), lambda i, be, nu: (be[i], 0, 0)),
                  pl.BlockSpec((1, D, F), lambda i, be, nu: (be[i], 0, 0)),
                  pl.BlockSpec((1, F, D), lambda i, be, nu: (be[i], 0, 0))],
        out_specs=pl.BlockSpec((tm, D), lambda i, be, nu: (i, 0)))
    return pl.pallas_call(
        _moe_kernel,
        grid_spec=grid_spec,
        out_shape=jax.ShapeDtypeStruct((R, D), F32),
        compiler_params=_cparams("arbitrary"),
        name="moe_experts",
    )(blk_e, n_used, x_sorted, wg, wu, wd)


def moe_ffn(h, p, tm):
    N = h.shape[0]
    hp = lax.Precision.HIGHEST
    lg = jnp.dot(h, p['moe_w_group'], precision=hp) + p['moe_b_group']
    g_sel = jnp.argmax(lg, axis=-1)
    pg_top = jnp.take_along_axis(jax.nn.softmax(lg, axis=-1), g_sel[:, None], axis=1)
    le = (jnp.dot(h, p['moe_w_expert'], precision=hp) + p['moe_b_expert']).reshape(N, N_GROUPS, EXPERTS_PER_GROUP)
    le_g = jnp.take_along_axis(le, g_sel[:, None, None], axis=1)[:, 0]
    v2, j2 = lax.top_k(jax.nn.softmax(le_g, axis=-1), TOP_K)
    w = pg_top * v2 / v2.sum(-1, keepdims=True)
    e_idx = (g_sel[:, None] * EXPERTS_PER_GROUP + j2).astype(jnp.int32)
    A = N * TOP_K
    e_flat = e_idx.reshape(A)
    onehot = (e_flat[:, None] == jnp.arange(N_EXPERTS, dtype=jnp.int32)[None, :]).astype(jnp.int32)
    counts = onehot.sum(0)
    padded = (counts + tm - 1) // tm * tm
    start = jnp.cumsum(counts) - counts
    pad_end = jnp.cumsum(padded)
    pad_start = pad_end - padded
    n_blk = (A + N_EXPERTS * (tm - 1)) // tm
    R = n_blk * tm
    order = jnp.argsort(e_flat, stable=True).astype(jnp.int32)
    r = jnp.arange(R, dtype=jnp.int32)
    e_r = jnp.minimum(jnp.searchsorted(pad_end, r, side='right'), N_EXPERTS - 1).astype(jnp.int32)
    off = r - pad_start[e_r]
    valid = off < counts[e_r]
    src = order[jnp.clip(start[e_r] + off, 0, A - 1)]
    row_tok = jnp.where(valid, src // TOP_K, 0)
    x_sorted = h.astype(BF16)[row_tok]
    blk_e = e_r[::tm]
    n_used = (pad_end[-1] // tm).astype(jnp.int32).reshape(1)
    y = moe_experts(x_sorted, blk_e, n_used, p['moe_w_gate_bf16'], p['moe_w_up_bf16'], p['moe_w_down_bf16'], tm)
    rank = jnp.take_along_axis(jnp.cumsum(onehot, axis=0) - onehot, e_flat[:, None], axis=1)[:, 0]
    dest = (pad_start[e_flat] + rank).reshape(N, TOP_K)
    out = y[dest[:, 0]] * w[:, 0:1]
    for k in range(1, TOP_K):
        out = out + y[dest[:, k]] * w[:, k:k + 1]
    return out


def layer_norm(x, g=None, b=None, eps=1e-5):
    mu = x.mean(-1, keepdims=True)
    var = jnp.mean(jnp.square(x - mu), -1, keepdims=True)
    y = (x - mu) * lax.rsqrt(var + eps)
    if g is not None:
        y = y * g + b
    return y


def rms_norm(x, g, eps=1e-6):
    return x * lax.rsqrt(jnp.mean(jnp.square(x), -1, keepdims=True) + eps) * g


def apply_rope(x, pos, n_rot, theta):
    half = n_rot // 2
    inv = theta ** (-jnp.arange(half, dtype=F32) / half)
    ang = pos.astype(F32)[:, None] * inv[None, :]
    shape = (1, pos.shape[0]) + (1,) * (x.ndim - 3) + (half,)
    cos, sin = jnp.cos(ang).reshape(shape), jnp.sin(ang).reshape(shape)
    xf = x[..., :n_rot]
    x1, x2 = xf[..., :half], xf[..., half:]
    rot = jnp.concatenate([x1 * cos - x2 * sin, x2 * cos + x1 * sin], -1)
    return jnp.concatenate([rot, x[..., n_rot:]], -1)


def masked_softmax(s, mask):
    s = jnp.where(mask, s, NEG_INF)
    return jnp.where(mask, jax.nn.softmax(s, axis=-1), 0.0)


def mixer_inputs(h, pos, p):
    B, T, _ = h.shape
    q_a, kv_a, g_a, qd_b, kvd_b, kr_b, g_m = jnp.split(mm(h, p['w_in']), IN_SPLITS, axis=-1)
    q_a = apply_rope(q_a.reshape(B, T, H_A, DH_A), pos, ROT_A, THETA_A)
    k_cmp, v_cmp, k_sel, v_sel, k_win, v_win = jnp.split(kv_a, 6, axis=-1)
    rope_a = lambda k: apply_rope(k, pos, ROT_A, THETA_A)
    q_b = mm(rms_norm(qd_b, p['mla_q_norm']), p['mla_w_uq']).reshape(B, T, H_B, NOPE_B + ROPE_B)
    q_lat = jnp.einsum('bthn,rhn->bthr', q_b[..., :NOPE_B], p['mla_w_uk'])
    q_rope = apply_rope(q_b[..., NOPE_B:], pos, ROPE_B, THETA_B)
    c_kv = rms_norm(kvd_b, p['mla_kv_norm'])
    k_rope = apply_rope(kr_b, pos, ROPE_B, THETA_B)
    return dict(
        q_a=q_a,
        g_a=jax.nn.sigmoid(g_a).reshape(B, T, H_A, 3),
        cmp_rows=jnp.concatenate([rope_a(k_cmp), v_cmp], -1),
        sel_rows=jnp.concatenate([rope_a(k_sel), v_sel], -1),
        win_rows=jnp.concatenate([rope_a(k_win), v_win], -1),
        q_lat=q_lat,
        q_rope=q_rope,
        mla_rows=jnp.concatenate([c_kv, k_rope], -1),
        g_m=jax.nn.sigmoid(g_m).reshape(B, T, 2, D_MODEL))


def mixer_output(o_a, o_lat, g_m, p):
    B, T = o_a.shape[:2]
    o_b = jnp.einsum('bthr,rhv->bthv', o_lat, p['mla_w_uv']).reshape(B, T, H_B * DV_B)
    y = g_m[:, :, 0] * mm(o_a, p['w_proj_a']) + g_m[:, :, 1] * mm(o_b, p['w_proj_b'])
    return mm(y, p['w_out'])


def mix_prompt(h, p):
    B, S, _ = h.shape
    m = mixer_inputs(h, jnp.arange(S), p)
    kvc = compress_rows(m['cmp_rows'], p['cmp_weights'])
    n_cmp = S // CMP_STRIDE - CMP_BLOCK // CMP_STRIDE + 1
    q = (m['q_a'] * (DH_A ** -0.5)).astype(BF16).transpose(0, 2, 1, 3)
    qp = jnp.concatenate([q, jnp.zeros_like(q)], axis=-1)
    o_c, sel = nsa_cmp_select(qp, kvc, n_cmp)
    o_s, o_w = nsa_sel_win(qp, m['sel_rows'].astype(BF16), m['win_rows'].astype(BF16), sel)
    g = m['g_a'].transpose(0, 2, 1, 3)
    o = (g[..., 0:1] * o_c[..., DH_A:] + g[..., 1:2] * o_s[..., DH_A:] + g[..., 2:3] * o_w[..., DH_A:])
    o_a = o.transpose(0, 2, 1, 3).reshape(B, S, H_A * DH_A)
    qc = jnp.concatenate([m['q_lat'], m['q_rope']], -1).astype(BF16).transpose(0, 2, 1, 3)
    o_lat = mla_prompt(qc, m['mla_rows'].astype(BF16)).transpose(0, 2, 1, 3)
    new_win = m['win_rows'][:, S - min(WINDOW, S):]
    return mixer_output(o_a, o_lat, m['g_m'], p), (m['cmp_rows'], m['sel_rows'], m['mla_rows'], new_win)


def nsa_compress_ref(rows, pos_emb, w1, w2):
    B, L, dh = rows.shape
    r = CMP_BLOCK // CMP_STRIDE
    n_chunk = L // CMP_STRIDE
    nc = n_chunk - r + 1
    chunks = rows[:, :n_chunk * CMP_STRIDE].reshape(B, n_chunk, CMP_STRIDE, dh)
    part = jnp.einsum('bncd,kcdf->bnkf', chunks, w1.reshape(r, CMP_STRIDE, dh, -1))
    pre = sum(part[:, k:k + nc, k] for k in range(r)) + pos_emb.reshape(-1) @ w1
    return jax.nn.gelu(pre) @ w2


def nsa_core(q, g, t_pos, kc, vc, gather_sel, kw, vw, kw_pos, n_sel_blk):
    B, T, H, dh = q.shape
    scale = dh ** -0.5
    nc = kc.shape[1]
    blk_last = jnp.arange(nc) * CMP_STRIDE + (CMP_BLOCK - 1)
    m_c = (blk_last[None, :] <= t_pos[:, None])[None, :, None, :]
    p_c = masked_softmax(jnp.einsum('bthd,bnd->bthn', q, kc) * scale, m_c)
    o_c = jnp.einsum('bthn,bnd->bthd', p_c, vc)
    ci = jnp.arange(nc)[:, None] * CMP_STRIDE
    sj = jnp.arange(n_sel_blk)[None, :] * SEL_BLOCK
    cover = ((ci < sj + SEL_BLOCK) & (ci + CMP_BLOCK > sj)).astype(F32)
    imp = p_c.sum(axis=2) @ cover
    j = jnp.arange(n_sel_blk)[None, :]
    cur = (t_pos // SEL_BLOCK)[:, None]
    forced = (j == 0) | (j == cur) | (j == cur - 1)
    score = jnp.where(j <= cur, imp + jnp.where(forced, FORCE_BONUS, 0.0), NEG_INF)
    k_sel = min(N_SEL, n_sel_blk)
    _, idx = lax.top_k(score, k_sel)
    blk_ok = idx <= cur[None]
    sel_pos = idx[..., None] * SEL_BLOCK + jnp.arange(SEL_BLOCK)
    m_s = (blk_ok[..., None] & (sel_pos <= t_pos[None, :, None, None])).reshape(B, T, 1, k_sel * SEL_BLOCK)
    ks, vs = gather_sel(sel_pos.reshape(B, T, k_sel * SEL_BLOCK))
    p_s = masked_softmax(jnp.einsum('bthd,btpd->bthp', q, ks) * scale, m_s)
    o_s = jnp.einsum('bthp,btpd->bthd', p_s, vs)
    dist = t_pos[:, None] - kw_pos[None, :]
    m_w = ((dist >= 0) & (dist < WINDOW) & (kw_pos[None, :] >= 0))[None, :, None, :]
    p_w = masked_softmax(jnp.einsum('bthd,bld->bthl', q, kw) * scale, m_w)
    o_w = jnp.einsum('bthl,bld->bthd', p_w, vw)
    o = g[..., 0:1] * o_c + g[..., 1:2] * o_s + g[..., 2:3] * o_w
    return o.reshape(B, T, H * dh)


def mla_core(q_lat, q_rope, t_pos, kv_lat, k_rope, kv_pos):
    s = (jnp.einsum('bthr,blr->bhtl', q_lat, kv_lat) + jnp.einsum('bthd,bld->bhtl', q_rope, k_rope)) * MLA_SCALE
    mask = (kv_pos[None, :] <= t_pos[:, None])[None, None]
    p = masked_softmax(s, mask)
    return jnp.einsum('bhtl,blr->bthr', p, kv_lat)


def mix_sample(h, p, l, cache_nsa_cmp, cache_nsa_sel, cache_mla, state_nsa_win, page_table):
    B, T, _ = h.shape
    pos = PAST_LEN + jnp.arange(T)
    m = mixer_inputs(h, pos, p)
    past_cmp = cache_nsa_cmp[l, page_table].reshape(B, PAST_LEN, 2 * DH_A)
    rows_c = jnp.concatenate([past_cmp, m['cmp_rows']], 1)
    kc = nsa_compress_ref(rows_c[..., :DH_A], p['nsa_cmp_pos_k'], p['nsa_cmp_w1_k'], p['nsa_cmp_w2_k'])
    vc = nsa_compress_ref(rows_c[..., DH_A:], p['nsa_cmp_pos_v'], p['nsa_cmp_w1_v'], p['nsa_cmp_w2_v'])
    b_idx = jnp.arange(B)[:, None, None]

    def gather_sel(sel_pos):
        in_past = sel_pos < PAST_LEN
        pp = jnp.clip(sel_pos, 0, PAST_LEN - 1)
        phys = page_table[b_idx, pp // PAGE_SIZE]
        rows_past = cache_nsa_sel[l, phys, pp % PAGE_SIZE]
        rows_new = m['sel_rows'][b_idx, jnp.clip(sel_pos - PAST_LEN, 0, T - 1)]
        rows = jnp.where(in_past[..., None], rows_past, rows_new)
        return rows[..., :DH_A], rows[..., DH_A:]

    win_len = state_nsa_win.shape[2]
    win = jnp.concatenate([state_nsa_win[l], m['win_rows']], 1)
    kw_pos = PAST_LEN - win_len + jnp.arange(win_len + T)
    n_sel_blk = -(-(PAST_LEN + T) // SEL_BLOCK)
    o_a = nsa_core(m['q_a'], m['g_a'], pos, kc, vc, gather_sel, win[..., :DH_A], win[..., DH_A:], kw_pos, n_sel_blk)
    past_mla = cache_mla[l, page_table].reshape(B, PAST_LEN, KV_RANK + ROPE_B)
    rows_mla = jnp.concatenate([past_mla, m['mla_rows']], 1)
    o_lat = mla_core(m['q_lat'], m['q_rope'], pos, rows_mla[..., :KV_RANK], rows_mla[..., KV_RANK:], jnp.arange(PAST_LEN + T))
    new_win = win[:, T:]
    return mixer_output(o_a, o_lat, m['g_m'], p), (m['cmp_rows'], m['sel_rows'], m['mla_rows'], new_win)


def trunk_layer(x, c, p, token_mixer, moe_rows):
    B, T, D = x.shape
    ada = (jax.nn.silu(c) @ p['w_ada'] + p['b_ada']).reshape(B, 6, 1, D)
    h = layer_norm(x) * (1.0 + ada[:, 1]) + ada[:, 0]
    mix, rows = token_mixer(h)
    x = layer_norm(ALPHA * x + ada[:, 2] * mix, p['ln1_g'], p['ln1_b'])
    h = layer_norm(x) * (1.0 + ada[:, 4]) + ada[:, 3]
    ff = moe_ffn(h.reshape(B * T, D), p, moe_rows).reshape(B, T, D)
    x = layer_norm(ALPHA * x + ada[:, 5] * ff, p['ln2_g'], p['ln2_b'])
    return x, rows


PARAM_NAMES = ('w_ada', 'b_ada', 'w_in', 'nsa_cmp_pos_k', 'nsa_cmp_w1_k', 'nsa_cmp_w2_k', 'nsa_cmp_pos_v',
               'nsa_cmp_w1_v', 'nsa_cmp_w2_v', 'mla_q_norm', 'mla_w_uq', 'mla_kv_norm', 'mla_w_uk', 'mla_w_uv',
               'w_proj_a', 'w_proj_b', 'w_out', 'ln1_g', 'ln1_b', 'moe_w_group', 'moe_b_group', 'moe_w_expert',
               'moe_b_expert', 'moe_w_gate', 'moe_w_up', 'moe_w_down', 'ln2_g', 'ln2_b')


def kernel(x_prompt, x_sample, cache_nsa_cmp, cache_nsa_sel, cache_mla, state_nsa_win, page_table, c_prompt, c_sample, w_ada, b_ada, w_in, nsa_cmp_pos_k, nsa_cmp_w1_k, nsa_cmp_w2_k, nsa_cmp_pos_v, nsa_cmp_w1_v, nsa_cmp_w2_v, mla_q_norm, mla_w_uq, mla_kv_norm, mla_w_uk, mla_w_uv, w_proj_a, w_proj_b, w_out, ln1_g, ln1_b, moe_w_group, moe_b_group, moe_w_expert, moe_b_expert, moe_w_gate, moe_w_up, moe_w_down, ln2_g, ln2_b):
    weights = dict(zip(PARAM_NAMES, (w_ada, b_ada, w_in, nsa_cmp_pos_k, nsa_cmp_w1_k, nsa_cmp_w2_k, nsa_cmp_pos_v,
                                     nsa_cmp_w1_v, nsa_cmp_w2_v, mla_q_norm, mla_w_uq, mla_kv_norm, mla_w_uk,
                                     mla_w_uv, w_proj_a, w_proj_b, w_out, ln1_g, ln1_b, moe_w_group, moe_b_group,
                                     moe_w_expert, moe_b_expert, moe_w_gate, moe_w_up, moe_w_down, ln2_g, ln2_b)))
    xp, xs = x_prompt, x_sample
    rows_p, rows_s = [], []
    for l in range(DEPTH):
        p = {k: v[l] for k, v in weights.items()}
        p['cmp_weights'] = compress_weights(p)
        for name in ('moe_w_gate', 'moe_w_up', 'moe_w_down'):
            p[name + '_bf16'] = p[name].astype(BF16)
        xp, rp = trunk_layer(xp, c_prompt, p, lambda h: mix_prompt(h, p), MOE_ROWS_PROMPT)
        xs, rs = trunk_layer(xs, c_sample, p, lambda h: mix_sample(
            h, p, l, cache_nsa_cmp, cache_nsa_sel, cache_mla, state_nsa_win, page_table), MOE_ROWS_SAMPLE)
        rows_p.append(rp)
        rows_s.append(rs)
    stack = lambda rows, i: jnp.stack([r[i] for r in rows])
    return (xp, xs,
            stack(rows_p, 0), stack(rows_s, 0),
            stack(rows_p, 1), stack(rows_s, 1),
            stack(rows_p, 2), stack(rows_s, 2),
            stack(rows_p, 3), stack(rows_s, 3))
```
